```python
import jax, jax.numpy as jnp
from jax import lax
import numpy as np

D_MODEL = 1024
BATCH = 8
SEQ = 2048
DEPTH = 1

D_RNN = D_MODEL
N_LRU_HEADS = 16
LRU_HEAD_DIM = D_RNN // N_LRU_HEADS
CONV_WIDTH = 4
LRU_C = 8.0
LRU_A_MIN = 0.9
LRU_A_MAX = 0.999
D_POOL = D_MODEL // 2
POOL_WINDOWS = (2, 4, 8, 16)
N_POOL_GROUPS = len(POOL_WINDOWS)
POOL_GROUP_DIM = D_POOL // N_POOL_GROUPS
N_BRANCHES = 2
D_FF = 4 * D_MODEL
NORM_EPS = 1e-6
D_IN = 2 * D_RNN + D_POOL + N_BRANCHES * D_MODEL

kernel_name = "hawk_pool_gated_hybrid_block"


def rms_norm(x, g):
    xf = x.astype(jnp.float32)
    y = xf * lax.rsqrt(jnp.mean(xf * xf, axis=-1, keepdims=True) + NORM_EPS)
    return (y * g.astype(jnp.float32)).astype(x.dtype)


def causal_depthwise_conv(x, w, b):
    S = x.shape[1]
    xp = jnp.pad(x, ((0, 0), (CONV_WIDTH - 1, 0), (0, 0)))
    out = b
    for k in range(CONV_WIDTH):
        out = out + xp[:, k:k + S] * w[k]
    return out


def rg_lru(x, w_a, b_a, w_x, b_x, lam):
    B, S, _ = x.shape
    xh = x.reshape(B, S, N_LRU_HEADS, LRU_HEAD_DIM)
    r = jnp.einsum('bshi,hij->bshj', xh, w_a).reshape(B, S, D_RNN) + b_a
    i = jnp.einsum('bshi,hij->bshj', xh, w_x).reshape(B, S, D_RNN) + b_x
    r = jax.nn.sigmoid(r.astype(jnp.float32))
    i = jax.nn.sigmoid(i.astype(jnp.float32))
    log_a = -LRU_C * r * jax.nn.softplus(-lam.astype(jnp.float32))
    a = jnp.exp(log_a)
    mult = jnp.sqrt(-jnp.expm1(2.0 * log_a))
    u = mult * (i * x.astype(jnp.float32))

    def combine(left, right):
        a1, b1 = left
        a2, b2 = right
        return a1 * a2, a2 * b1 + b2

    _, h = lax.associative_scan(combine, (a, u), axis=1)
    return h.astype(x.dtype)


def multiscale_pool(x, w_grp, scale):
    B, S, _ = x.shape
    xf = x.astype(jnp.float32)
    cs = jnp.cumsum(xf, axis=1)
    pos = jnp.arange(1, S + 1, dtype=jnp.float32)[None, :, None]
    outs = []
    for g, w in enumerate(POOL_WINDOWS):
        sl = slice(g * POOL_GROUP_DIM, (g + 1) * POOL_GROUP_DIM)
        c = cs[..., sl]
        c_prev = jnp.pad(c, ((0, 0), (w, 0), (0, 0)))[:, :S]
        cnt = jnp.minimum(pos, float(w))
        outs.append((c - c_prev) / cnt - xf[..., sl])
    p = jnp.stack(outs, axis=2)
    y = jnp.einsum('bsgi,gij->bsgj', p, w_grp.astype(jnp.float32)).reshape(B, S, D_POOL)
    return (y * scale.astype(jnp.float32)).astype(x.dtype)


def hybrid_mixer(h, w_in, b_gate, conv_w, conv_b, lru_w_a, lru_b_a, lru_w_x, lru_b_x,
                 lru_lambda, pool_w, pool_scale, w_lru_up, w_pool_up, w_o):
    proj = h @ w_in
    x_lru, g_lru, x_pool, gates = jnp.split(
        proj, [D_RNN, 2 * D_RNN, 2 * D_RNN + D_POOL], axis=-1)
    x_lru = causal_depthwise_conv(x_lru, conv_w, conv_b)
    y_lru = rg_lru(x_lru, lru_w_a, lru_b_a, lru_w_x, lru_b_x, lru_lambda) * jax.nn.gelu(g_lru)
    y_pool = multiscale_pool(x_pool, pool_w, pool_scale)
    br_a = y_lru @ w_lru_up
    br_b = y_pool @ w_pool_up
    gate_a, gate_b = jnp.split(jax.nn.sigmoid(gates + b_gate), N_BRANCHES, axis=-1)
    return (gate_a * br_a + gate_b * br_b) @ w_o


def sq_relu_mlp(h, w_ff1, w_ff2):
    return jnp.square(jax.nn.relu(h @ w_ff1)) @ w_ff2


def setup_inputs(seed: int = 0) -> dict:
    key = jax.random.key(seed)
    ks = jax.random.split(key, 24)
    L = DEPTH
    f32 = jnp.float32

    def nrm(k, shape, fan_in):
        return jax.random.normal(k, shape, f32) * (fan_in ** -0.5)

    def gain(k, shape):
        return 1.0 + 0.02 * jax.random.normal(k, shape, f32)

    def bias(k, shape):
        return 0.01 * jax.random.normal(k, shape, f32)

    u = jax.random.uniform(ks[12], (L, D_RNN), f32, LRU_A_MIN, LRU_A_MAX)
    a_base = u ** (1.0 / LRU_C)
    lru_lambda = jnp.log(a_base) - jnp.log1p(-a_base)

    return {
        "x": jax.random.normal(ks[0], (BATCH, SEQ, D_MODEL), f32),
        "norm_mix_pre": gain(ks[1], (L, D_MODEL)),
        "norm_mix_post": gain(ks[2], (L, D_MODEL)),
        "norm_mlp_pre": gain(ks[3], (L, D_MODEL)),
        "norm_mlp_post": gain(ks[4], (L, D_MODEL)),
        "w_in": nrm(ks[5], (L, D_MODEL, D_IN), D_MODEL),
        "b_gate": bias(ks[6], (L, N_BRANCHES * D_MODEL)),
        "conv_w": nrm(ks[7], (L, CONV_WIDTH, D_RNN), CONV_WIDTH),
        "conv_b": bias(ks[8], (L, D_RNN)),
        "lru_w_a": nrm(ks[9], (L, N_LRU_HEADS, LRU_HEAD_DIM, LRU_HEAD_DIM), LRU_HEAD_DIM),
        "lru_b_a": bias(ks[10], (L, D_RNN)),
        "lru_w_x": nrm(ks[11], (L, N_LRU_HEADS, LRU_HEAD_DIM, LRU_HEAD_DIM), LRU_HEAD_DIM),
        "lru_b_x": bias(ks[13], (L, D_RNN)),
        "lru_lambda": lru_lambda,
        "pool_w": nrm(ks[14], (L, N_POOL_GROUPS, POOL_GROUP_DIM, POOL_GROUP_DIM), POOL_GROUP_DIM),
        "pool_scale": gain(ks[15], (L, D_POOL)),
        "w_lru_up": nrm(ks[16], (L, D_RNN, D_MODEL), D_RNN),
        "w_pool_up": nrm(ks[17], (L, D_POOL, D_MODEL), D_POOL),
        "w_o": nrm(ks[18], (L, D_MODEL, D_MODEL), D_MODEL),
        "w_ff1": nrm(ks[19], (L, D_MODEL, D_FF), D_MODEL),
        "w_ff2": nrm(ks[20], (L, D_FF, D_MODEL), D_FF),
    }


def reference(x, norm_mix_pre, norm_mix_post, norm_mlp_pre, norm_mlp_post, w_in, b_gate,
              conv_w, conv_b, lru_w_a, lru_b_a, lru_w_x, lru_b_x, lru_lambda, pool_w,
              pool_scale, w_lru_up, w_pool_up, w_o, w_ff1, w_ff2):
    for l in range(DEPTH):
        h = rms_norm(x, norm_mix_pre[l])
        m = hybrid_mixer(h, w_in[l], b_gate[l], conv_w[l], conv_b[l], lru_w_a[l], lru_b_a[l],
                         lru_w_x[l], lru_b_x[l], lru_lambda[l], pool_w[l], pool_scale[l],
                         w_lru_up[l], w_pool_up[l], w_o[l])
        x = x + rms_norm(m, norm_mix_post[l])
        h = rms_norm(x, norm_mlp_pre[l])
        f = sq_relu_mlp(h, w_ff1[l], w_ff2[l])
        x = x + rms_norm(f, norm_mlp_post[l])
    return x
```

```python
import functools

import jax
import jax.numpy as jnp
from jax import lax
from jax.experimental import pallas as pl
from jax.experimental.pallas import tpu as pltpu

NORM_EPS = 1e-6
LRU_C = 8.0
CONV_WIDTH = 4
POOL_WINDOWS = (2, 4, 8, 16)
N_LRU_HEADS = 16

V7X_SUBLANES = 8
V7X_LANES = 128
V7X_MXU_DIM = 256
V7X_VMEM_BYTES = 64 * 1024 * 1024

MIXER_TILE_STEPS = 64
MLP_TILE_ROWS = 512
MLP_FF_CHUNK = 1024


def _dot(a, b):
    return jnp.dot(a, b, preferred_element_type=jnp.float32)


def _rms_scale(v):
    return lax.rsqrt(jnp.mean(v * v, axis=-1, keepdims=True) + NORM_EPS)


def _mixer_kernel(x_ref, gpre_ref, gpost_ref, win_ref, bgate_ref, convw_ref, convb_ref,
                  wax_ref, ba_ref, bx_ref, lam_ref, poolw_ref, pscale_ref, wlru_ref,
                  wpool_ref, wo_ref, out_ref,
                  xl_buf, xp_buf, a_buf, u_buf, h_state, *, batch, steps):
    rows = batch * steps
    d_rnn = wlru_ref.shape[0]
    d_pool = wpool_ref.shape[0]
    d_model = x_ref.shape[1]
    conv_tail = (CONV_WIDTH - 1) * batch
    pool_tail = (max(POOL_WINDOWS) - 1) * batch
    group_dim = d_pool // len(POOL_WINDOWS)
    pid = pl.program_id(0)

    @pl.when(pid == 0)
    def _():
        xl_buf[0:conv_tail, :] = jnp.zeros((conv_tail, d_rnn), jnp.float32)
        xp_buf[0:pool_tail, :] = jnp.zeros((pool_tail, d_pool), jnp.float32)
        h_state[...] = jnp.zeros_like(h_state)

    xt = x_ref[...]
    h = (xt * _rms_scale(xt) * gpre_ref[...]).astype(jnp.bfloat16)

    xl_buf[conv_tail:conv_tail + rows, :] = _dot(h, win_ref[:, 0:d_rnn])
    xc = convb_ref[...]
    for k in range(CONV_WIDTH):
        xc = xc + xl_buf[k * batch:k * batch + rows, :] * convw_ref[k:k + 1, :]
    xl_buf[0:conv_tail, :] = xl_buf[rows:rows + conv_tail, :]

    neg_c_softplus = -LRU_C * jax.nn.softplus(-lam_ref[...])
    xcb = xc.astype(jnp.bfloat16)
    blk = V7X_MXU_DIM
    for j in range(d_rnn // blk):
        sl = slice(j * blk, (j + 1) * blk)
        ri = _dot(xcb[:, sl], wax_ref[j])
        r = jax.nn.sigmoid(ri[:, :blk] + ba_ref[:, sl])
        i = jax.nn.sigmoid(ri[:, blk:] + bx_ref[:, sl])
        log_a = r * neg_c_softplus[:, sl]
        a = jnp.exp(log_a)
        a_buf[:, sl] = a
        one_minus_a2 = -jnp.tanh(log_a) * (a * a + 1.0)
        u_buf[:, sl] = jnp.sqrt(one_minus_a2) * (i * xc[:, sl])

    def scan_step(t, hprev):
        r0 = pl.multiple_of(t * batch, batch)
        hnew = a_buf[pl.ds(r0, batch), :] * hprev + u_buf[pl.ds(r0, batch), :]
        u_buf[pl.ds(r0, batch), :] = hnew
        return hnew

    h_state[...] = lax.fori_loop(0, steps, scan_step, h_state[...], unroll=8)

    g_lru = _dot(h, win_ref[:, d_rnn:2 * d_rnn])
    y_lru = (u_buf[...] * jax.nn.gelu(g_lru)).astype(jnp.bfloat16)
    br_a = _dot(y_lru, wlru_ref[...])

    xp_buf[pool_tail:pool_tail + rows, :] = _dot(h, win_ref[:, 2 * d_rnn:2 * d_rnn + d_pool])
    row = lax.broadcasted_iota(jnp.int32, (rows, 1), 0)
    pos = pid * steps + lax.shift_right_logical(row, batch.bit_length() - 1) + 1
    pooled = []
    for g, w in enumerate(POOL_WINDOWS):
        gs = slice(g * group_dim, (g + 1) * group_dim)
        cur = xp_buf[pool_tail:pool_tail + rows, gs]
        s = cur
        for k in range(1, w):
            off = pool_tail - k * batch
            s = s + xp_buf[off:off + rows, gs]
        inv_cnt = 1.0 / jnp.minimum(pos, w).astype(jnp.float32)
        pooled.append(s * inv_cnt - cur)
    xp_buf[0:pool_tail, :] = xp_buf[rows:rows + pool_tail, :]
    pb = jnp.concatenate(pooled, axis=-1).astype(jnp.bfloat16)
    y_pool = jnp.concatenate(
        [_dot(pb[:, j * blk:(j + 1) * blk], poolw_ref[j]) for j in range(d_pool // blk)], axis=-1)
    y_pool = (y_pool * pscale_ref[...]).astype(jnp.bfloat16)
    br_b = _dot(y_pool, wpool_ref[...])

    gates = _dot(h, win_ref[:, 2 * d_rnn + d_pool:]) + bgate_ref[...]
    mix = jax.nn.sigmoid(gates[:, :d_model]) * br_a + jax.nn.sigmoid(gates[:, d_model:]) * br_b
    m = _dot(mix.astype(jnp.bfloat16), wo_ref[...])
    out_ref[...] = xt + m * _rms_scale(m) * gpost_ref[...]


def _mlp_kernel(x_ref, gpre_ref, gpost_ref, w1_ref, w2_ref, out_ref, hid_ref):
    xt = x_ref[...]
    h = (xt * _rms_scale(xt) * gpre_ref[...]).astype(jnp.bfloat16)
    d_ff = w1_ref.shape[1]
    for c in range(d_ff // MLP_FF_CHUNK):
        sl = slice(c * MLP_FF_CHUNK, (c + 1) * MLP_FF_CHUNK)
        a = jnp.maximum(_dot(h, w1_ref[:, sl]), 0.0)
        hid_ref[:, sl] = (a * a).astype(jnp.bfloat16)
    f = _dot(hid_ref[...], w2_ref[...])
    out_ref[...] = xt + f * _rms_scale(f) * gpost_ref[...]


def _resident(shape):
    nd = len(shape)
    return pl.BlockSpec(shape, lambda i: (0,) * nd, pipeline_mode=pl.Buffered(1))


def _nbytes(shape, dtype):
    n = 1
    for s in shape:
        n *= s
    return n * jnp.dtype(dtype).itemsize


def _vmem_limit(resident, streamed, scratch, temporaries):
    need = sum(resident) + 2 * sum(streamed) + sum(scratch) + sum(temporaries)
    return min(need, V7X_VMEM_BYTES)


def _block_diag(w, per_block):
    n_heads, hd, _ = w.shape
    wg = w.reshape(n_heads // per_block, per_block, hd, hd)
    eye = jnp.eye(per_block, dtype=w.dtype)
    bd = jnp.einsum('ghij,hk->ghikj', wg, eye)
    return bd.reshape(n_heads // per_block, per_block * hd, per_block * hd)


def _mixer_call(xs, p, *, batch):
    n_rows, d_model = xs.shape
    steps = MIXER_TILE_STEPS
    rows = batch * steps
    assert batch == V7X_SUBLANES and n_rows % rows == 0
    d_rnn, d_pool = p["wlru"].shape[0], p["wpool"].shape[0]
    conv_tail = (CONV_WIDTH - 1) * batch
    pool_tail = (max(POOL_WINDOWS) - 1) * batch
    assert rows >= pool_tail

    weights = [p["gpre"], p["gpost"], p["win"], p["bgate"], p["convw"], p["convb"], p["wax"],
               p["ba"], p["bx"], p["lam"], p["poolw"], p["pscale"], p["wlru"], p["wpool"], p["wo"]]
    row_spec = pl.BlockSpec((rows, d_model), lambda i: (i, 0))
    scratch = [((conv_tail + rows, d_rnn), jnp.float32), ((pool_tail + rows, d_pool), jnp.float32),
               ((rows, d_rnn), jnp.float32), ((rows, d_rnn), jnp.float32),
               ((batch, d_rnn), jnp.float32)]
    f32_row = _nbytes((rows, d_model), jnp.float32)
    vmem = _vmem_limit(
        resident=[_nbytes(w.shape, w.dtype) for w in weights],
        streamed=[f32_row, f32_row],
        scratch=[_nbytes(s, d) for s, d in scratch],
        temporaries=[8 * f32_row])
    return pl.pallas_call(
        functools.partial(_mixer_kernel, batch=batch, steps=steps),
        grid=(n_rows // rows,),
        in_specs=[row_spec] + [_resident(w.shape) for w in weights],
        out_specs=row_spec,
        out_shape=jax.ShapeDtypeStruct(xs.shape, xs.dtype),
        scratch_shapes=[pltpu.VMEM(s, d) for s, d in scratch],
        compiler_params=pltpu.CompilerParams(
            dimension_semantics=("arbitrary",), vmem_limit_bytes=vmem),
        name="mixer_call",
    )(xs, *weights)


def _mlp_call(xs, gpre, gpost, w1, w2):
    n_rows, d_model = xs.shape
    rows = MLP_TILE_ROWS
    assert n_rows % rows == 0 and w1.shape[1] % MLP_FF_CHUNK == 0
    weights = [gpre, gpost, w1, w2]
    row_spec = pl.BlockSpec((rows, d_model), lambda i: (i, 0))
    hid = ((rows, w1.shape[1]), jnp.bfloat16)
    f32_row = _nbytes((rows, d_model), jnp.float32)
    vmem = _vmem_limit(
        resident=[_nbytes(w.shape, w.dtype) for w in weights],
        streamed=[f32_row, f32_row],
        scratch=[_nbytes(*hid)],
        temporaries=[_nbytes((rows, MLP_FF_CHUNK), jnp.float32) * 2, 4 * f32_row])
    return pl.pallas_call(
        _mlp_kernel,
        grid=(n_rows // rows,),
        in_specs=[row_spec] + [_resident(w.shape) for w in weights],
        out_specs=row_spec,
        out_shape=jax.ShapeDtypeStruct(xs.shape, xs.dtype),
        scratch_shapes=[pltpu.VMEM(*hid)],
        compiler_params=pltpu.CompilerParams(
            dimension_semantics=("arbitrary",), vmem_limit_bytes=vmem),
        name="mlp_call",
    )(xs, *weights)


def kernel(x, norm_mix_pre, norm_mix_post, norm_mlp_pre, norm_mlp_post, w_in, b_gate, conv_w, conv_b, lru_w_a, lru_b_a, lru_w_x, lru_b_x, lru_lambda, pool_w, pool_scale, w_lru_up, w_pool_up, w_o, w_ff1, w_ff2):
    batch, seq, d_model = x.shape
    depth = w_in.shape[0]
    bf16 = jnp.bfloat16
    heads_per_block = V7X_MXU_DIM // lru_w_a.shape[-1]
    groups_per_block = V7X_MXU_DIM // pool_w.shape[-1]
    row2 = lambda v: v.reshape(1, -1)

    xs = jnp.transpose(x, (1, 0, 2)).reshape(seq * batch, d_model)
    for l in range(depth):
        wax = jnp.concatenate([_block_diag(lru_w_a[l], heads_per_block),
                               _block_diag(lru_w_x[l], heads_per_block)], axis=-1).astype(bf16)
        p = dict(gpre=row2(norm_mix_pre[l]), gpost=row2(norm_mix_post[l]), win=w_in[l].astype(bf16),
                 bgate=row2(b_gate[l]), convw=conv_w[l], convb=row2(conv_b[l]), wax=wax,
                 ba=row2(lru_b_a[l]), bx=row2(lru_b_x[l]), lam=row2(lru_lambda[l]),
                 poolw=_block_diag(pool_w[l], groups_per_block).astype(bf16),
                 pscale=row2(pool_scale[l]), wlru=w_lru_up[l].astype(bf16),
                 wpool=w_pool_up[l].astype(bf16), wo=w_o[l].astype(bf16))
        xs = _mixer_call(xs, p, batch=batch)
        xs = _mlp_call(xs, row2(norm_mlp_pre[l]), row2(norm_mlp_post[l]),
                       w_ff1[l].astype(bf16), w_ff2[l].astype(bf16))
    return jnp.transpose(xs.reshape(seq, batch, d_model), (1, 0, 2))
```

```python
import functools

import jax
import jax.numpy as jnp
from jax import lax
from jax.experimental import pallas as pl
from jax.experimental.pallas import tpu as pltpu

NORM_EPS = 1e-6
LRU_C = 8.0
CONV_WIDTH = 4
POOL_WINDOWS = (2, 4, 8, 16)
N_LRU_HEADS = 16

V7X_SUBLANES = 8
V7X_LANES = 128
V7X_MXU_DIM = 256
V7X_VMEM_BYTES = 64 * 1024 * 1024

MIXER_TILE_STEPS = 64
MLP_TILE_ROWS = 512
MLP_FF_CHUNK = 1024


def _dot(a, b):
    return jnp.dot(a, b, preferred_element_type=jnp.float32)


def _rms_scale(v):
    return lax.rsqrt(jnp.mean(v * v, axis=-1, keepdims=True) + NORM_EPS)


def _mixer_kernel(x_ref, gpre_ref, gpost_ref, win_ref, bgate_ref, convw_ref, convb_ref,
                  wax_ref, ba_ref, bx_ref, lam_ref, poolw_ref, pscale_ref, wlru_ref,
                  wpool_ref, wo_ref, out_ref,
                  xl_buf, xp_buf, a_buf, u_buf, h_state, *, batch, steps):
    rows = batch * steps
    d_rnn = wlru_ref.shape[0]
    d_pool = wpool_ref.shape[0]
    d_model = x_ref.shape[-1]
    conv_tail = (CONV_WIDTH - 1) * batch
    pool_tail = (max(POOL_WINDOWS) - 1) * batch
    group_dim = d_pool // len(POOL_WINDOWS)
    pid = pl.program_id(0)

    @pl.when(pid == 0)
    def _():
        xl_buf[0:conv_tail, :] = jnp.zeros((conv_tail, d_rnn), jnp.float32)
        xp_buf[0:pool_tail, :] = jnp.zeros((pool_tail, d_pool), jnp.float32)
        h_state[...] = jnp.zeros_like(h_state)

    xt = jnp.swapaxes(x_ref[...], 0, 1).reshape(rows, d_model)
    h = (xt * _rms_scale(xt) * gpre_ref[...]).astype(jnp.bfloat16)

    xl_buf[conv_tail:conv_tail + rows, :] = _dot(h, win_ref[:, 0:d_rnn])
    xc = convb_ref[...]
    for k in range(CONV_WIDTH):
        xc = xc + xl_buf[k * batch:k * batch + rows, :] * convw_ref[k:k + 1, :]
    xl_buf[0:conv_tail, :] = xl_buf[rows:rows + conv_tail, :]
    xcb = xc.astype(jnp.bfloat16)
    blk = V7X_MXU_DIM
    n_blk = d_rnn // blk
    ris = [_dot(xcb[:, j * blk:(j + 1) * blk], wax_ref[j]) for j in range(n_blk)]

    g_lru = _dot(h, win_ref[:, d_rnn:2 * d_rnn])
    xp_buf[pool_tail:pool_tail + rows, :] = _dot(h, win_ref[:, 2 * d_rnn:2 * d_rnn + d_pool])
    gates = _dot(h, win_ref[:, 2 * d_rnn + d_pool:]) + bgate_ref[...]

    neg_c_softplus = -LRU_C * jax.nn.softplus(-lam_ref[...])
    for j in range(n_blk):
        sl = slice(j * blk, (j + 1) * blk)
        ri = ris[j]
        r = jax.nn.sigmoid(ri[:, :blk] + ba_ref[:, sl])
        i = jax.nn.sigmoid(ri[:, blk:] + bx_ref[:, sl])
        log_a = r * neg_c_softplus[:, sl]
        a = jnp.exp(log_a)
        a_buf[:, sl] = a
        one_minus_a2 = -jnp.tanh(log_a) * (a * a + 1.0)
        u_buf[:, sl] = jnp.sqrt(one_minus_a2) * (i * xc[:, sl])

    hs = h_state[...]
    for t in range(steps):
        ts = slice(t * batch, (t + 1) * batch)
        hs = a_buf[ts, :] * hs + u_buf[ts, :]
        u_buf[ts, :] = hs
    h_state[...] = hs

    y_lru = (u_buf[...] * jax.nn.gelu(g_lru)).astype(jnp.bfloat16)
    br_a = _dot(y_lru, wlru_ref[...])

    row = lax.broadcasted_iota(jnp.int32, (rows, 1), 0)
    pos = pid * steps + lax.shift_right_logical(row, batch.bit_length() - 1) + 1
    pooled = []
    for g, w in enumerate(POOL_WINDOWS):
        gs = slice(g * group_dim, (g + 1) * group_dim)
        cur = xp_buf[pool_tail:pool_tail + rows, gs]
        s = cur
        for k in range(1, w):
            off = pool_tail - k * batch
            s = s + xp_buf[off:off + rows, gs]
        inv_cnt = 1.0 / jnp.minimum(pos, w).astype(jnp.float32)
        pooled.append(s * inv_cnt - cur)
    xp_buf[0:pool_tail, :] = xp_buf[rows:rows + pool_tail, :]
    pb = jnp.concatenate(pooled, axis=-1).astype(jnp.bfloat16)
    y_pool = jnp.concatenate(
        [_dot(pb[:, j * blk:(j + 1) * blk], poolw_ref[j]) for j in range(d_pool // blk)], axis=-1)
    y_pool = (y_pool * pscale_ref[...]).astype(jnp.bfloat16)
    br_b = _dot(y_pool, wpool_ref[...])

    mix = jax.nn.sigmoid(gates[:, :d_model]) * br_a + jax.nn.sigmoid(gates[:, d_model:]) * br_b
    m = _dot(mix.astype(jnp.bfloat16), wo_ref[...])
    out = xt + m * _rms_scale(m) * gpost_ref[...]
    out_ref[...] = jnp.swapaxes(out.reshape(steps, batch, d_model), 0, 1)


def _mlp_kernel(x_ref, gpre_ref, gpost_ref, w1_ref, w2_ref, out_ref, hid_ref):
    xt = x_ref[...]
    h = (xt * _rms_scale(xt) * gpre_ref[...]).astype(jnp.bfloat16)
    d_ff = w1_ref.shape[1]
    for c in range(d_ff // MLP_FF_CHUNK):
        sl = slice(c * MLP_FF_CHUNK, (c + 1) * MLP_FF_CHUNK)
        a = jnp.maximum(_dot(h, w1_ref[:, sl]), 0.0)
        hid_ref[:, sl] = (a * a).astype(jnp.bfloat16)
    f = _dot(hid_ref[...], w2_ref[...])
    out_ref[...] = xt + f * _rms_scale(f) * gpost_ref[...]


def _resident(shape):
    nd = len(shape)
    return pl.BlockSpec(shape, lambda i: (0,) * nd, pipeline_mode=pl.Buffered(1))


def _nbytes(shape, dtype):
    n = 1
    for s in shape:
        n *= s
    return n * jnp.dtype(dtype).itemsize


def _vmem_limit(resident, streamed, scratch, temporaries):
    need = sum(resident) + 2 * sum(streamed) + sum(scratch) + sum(temporaries)
    return min(need, V7X_VMEM_BYTES)


def _block_diag(w, per_block):
    n_heads, hd, _ = w.shape
    wg = w.reshape(n_heads // per_block, per_block, hd, hd)
    eye = jnp.eye(per_block, dtype=w.dtype)
    bd = jnp.einsum('ghij,hk->ghikj', wg, eye)
    return bd.reshape(n_heads // per_block, per_block * hd, per_block * hd)


def _mixer_call(x, p):
    batch, seq, d_model = x.shape
    steps = MIXER_TILE_STEPS
    rows = batch * steps
    assert batch == V7X_SUBLANES and seq % steps == 0
    d_rnn, d_pool = p["wlru"].shape[0], p["wpool"].shape[0]
    conv_tail = (CONV_WIDTH - 1) * batch
    pool_tail = (max(POOL_WINDOWS) - 1) * batch
    assert rows >= pool_tail

    weights = [p["gpre"], p["gpost"], p["win"], p["bgate"], p["convw"], p["convb"], p["wax"],
               p["ba"], p["bx"], p["lam"], p["poolw"], p["pscale"], p["wlru"], p["wpool"], p["wo"]]
    row_spec = pl.BlockSpec((batch, steps, d_model), lambda i: (0, i, 0))
    scratch = [((conv_tail + rows, d_rnn), jnp.float32), ((pool_tail + rows, d_pool), jnp.float32),
               ((rows, d_rnn), jnp.float32), ((rows, d_rnn), jnp.float32),
               ((batch, d_rnn), jnp.float32)]
    f32_row = _nbytes((rows, d_model), jnp.float32)
    vmem = _vmem_limit(
        resident=[_nbytes(w.shape, w.dtype) for w in weights],
        streamed=[f32_row, f32_row],
        scratch=[_nbytes(s, d) for s, d in scratch],
        temporaries=[8 * f32_row])
    return pl.pallas_call(
        functools.partial(_mixer_kernel, batch=batch, steps=steps),
        grid=(seq // steps,),
        in_specs=[row_spec] + [_resident(w.shape) for w in weights],
        out_specs=row_spec,
        out_shape=jax.ShapeDtypeStruct(x.shape, x.dtype),
        scratch_shapes=[pltpu.VMEM(s, d) for s, d in scratch],
        compiler_params=pltpu.CompilerParams(
            dimension_semantics=("arbitrary",), vmem_limit_bytes=vmem),
        name="mixer_call",
    )(x, *weights)


def _mlp_call(xs, gpre, gpost, w1, w2):
    n_rows, d_model = xs.shape
    rows = MLP_TILE_ROWS
    assert n_rows % rows == 0 and w1.shape[1] % MLP_FF_CHUNK == 0
    weights = [gpre, gpost, w1, w2]
    row_spec = pl.BlockSpec((rows, d_model), lambda i: (i, 0))
    hid = ((rows, w1.shape[1]), jnp.bfloat16)
    f32_row = _nbytes((rows, d_model), jnp.float32)
    vmem = _vmem_limit(
        resident=[_nbytes(w.shape, w.dtype) for w in weights],
        streamed=[f32_row, f32_row],
        scratch=[_nbytes(*hid)],
        temporaries=[_nbytes((rows, MLP_FF_CHUNK), jnp.float32) * 2, 4 * f32_row])
    return pl.pallas_call(
        _mlp_kernel,
        grid=(n_rows // rows,),
        in_specs=[row_spec] + [_resident(w.shape) for w in weights],
        out_specs=row_spec,
        out_shape=jax.ShapeDtypeStruct(xs.shape, xs.dtype),
        scratch_shapes=[pltpu.VMEM(*hid)],
        compiler_params=pltpu.CompilerParams(
            dimension_semantics=("arbitrary",), vmem_limit_bytes=vmem),
        name="mlp_call",
    )(xs, *weights)


def kernel(x, norm_mix_pre, norm_mix_post, norm_mlp_pre, norm_mlp_post, w_in, b_gate, conv_w, conv_b, lru_w_a, lru_b_a, lru_w_x, lru_b_x, lru_lambda, pool_w, pool_scale, w_lru_up, w_pool_up, w_o, w_ff1, w_ff2):
    batch, seq, d_model = x.shape
    depth = w_in.shape[0]
    bf16 = jnp.bfloat16
    heads_per_block = V7X_MXU_DIM // lru_w_a.shape[-1]
    groups_per_block = V7X_MXU_DIM // pool_w.shape[-1]
    row2 = lambda v: v.reshape(1, -1)

    for l in range(depth):
        wax = jnp.concatenate([_block_diag(lru_w_a[l], heads_per_block),
                               _block_diag(lru_w_x[l], heads_per_block)], axis=-1).astype(bf16)
        p = dict(gpre=row2(norm_mix_pre[l]), gpost=row2(norm_mix_post[l]), win=w_in[l].astype(bf16),
                 bgate=row2(b_gate[l]), convw=conv_w[l], convb=row2(conv_b[l]), wax=wax,
                 ba=row2(lru_b_a[l]), bx=row2(lru_b_x[l]), lam=row2(lru_lambda[l]),
                 poolw=_block_diag(pool_w[l], groups_per_block).astype(bf16),
                 pscale=row2(pool_scale[l]), wlru=w_lru_up[l].astype(bf16),
                 wpool=w_pool_up[l].astype(bf16), wo=w_o[l].astype(bf16))
        x = _mixer_call(x, p)
        x = _mlp_call(x.reshape(batch * seq, d_model), row2(norm_mlp_pre[l]), row2(norm_mlp_post[l]),
                      w_ff1[l].astype(bf16), w_ff2[l].astype(bf16)).reshape(batch, seq, d_model)
    return x
```

```python
import functools

import jax
import jax.numpy as jnp
from jax import lax
from jax.experimental import pallas as pl
from jax.experimental.pallas import tpu as pltpu

NORM_EPS = 1e-6
LRU_C = 8.0
CONV_WIDTH = 4
POOL_WINDOWS = (2, 4, 8, 16)
N_LRU_HEADS = 16

V7X_SUBLANES = 8
V7X_LANES = 128
V7X_MXU_DIM = 256
V7X_VMEM_BYTES = 64 * 1024 * 1024

MIXER_TILE_STEPS = 64
MLP_TILE_ROWS = 1024
MLP_SUBTILES = 2
MLP_FF_CHUNK = 1024


def _dot(a, b):
    return jnp.dot(a, b, preferred_element_type=jnp.float32)


def _gelu_tanh(v):
    c = 0.7978845608028654
    half = 0.5 * v
    return half * jnp.tanh(v * (c + (c * 0.044715) * (v * v))) + half


def _rms_scale(v):
    return lax.rsqrt(jnp.mean(v * v, axis=-1, keepdims=True) + NORM_EPS)


def _mixer_kernel(x_ref, gpre_ref, gpost_ref, win_ref, bgate_ref, convw_ref, convb_ref,
                  wax_ref, ba_ref, bx_ref, lam_ref, poolw_ref, pscale_ref, wlru_ref,
                  wpool_ref, wo_ref, out_ref,
                  xl_buf, xp_buf, a_buf, u_buf, h_state, *, batch, steps):
    rows = batch * steps
    d_rnn = wlru_ref.shape[0]
    d_pool = wpool_ref.shape[0]
    d_model = x_ref.shape[-1]
    conv_tail = (CONV_WIDTH - 1) * batch
    pool_tail = (max(POOL_WINDOWS) - 1) * batch
    group_dim = d_pool // len(POOL_WINDOWS)
    pid = pl.program_id(0)

    @pl.when(pid == 0)
    def _():
        xl_buf[0:conv_tail, :] = jnp.zeros((conv_tail, d_rnn), jnp.float32)
        xp_buf[0:pool_tail, :] = jnp.zeros((pool_tail, d_pool), jnp.float32)
        h_state[...] = jnp.zeros_like(h_state)

    xt = jnp.swapaxes(x_ref[...], 0, 1).reshape(rows, d_model)
    h = (xt * _rms_scale(xt) * gpre_ref[...]).astype(jnp.bfloat16)

    xl_buf[conv_tail:conv_tail + rows, :] = _dot(h, win_ref[:, 0:d_rnn])
    xc = convb_ref[...]
    for k in range(CONV_WIDTH):
        xc = xc + xl_buf[k * batch:k * batch + rows, :] * convw_ref[k:k + 1, :]
    xl_buf[0:conv_tail, :] = xl_buf[rows:rows + conv_tail, :]
    xcb = xc.astype(jnp.bfloat16)
    blk = V7X_MXU_DIM
    n_blk = d_rnn // blk
    ris = [_dot(xcb[:, j * blk:(j + 1) * blk], wax_ref[j]) for j in range(n_blk)]

    g_lru = _dot(h, win_ref[:, d_rnn:2 * d_rnn])
    xp_buf[pool_tail:pool_tail + rows, :] = _dot(h, win_ref[:, 2 * d_rnn:2 * d_rnn + d_pool])
    gates = _dot(h, win_ref[:, 2 * d_rnn + d_pool:]) + bgate_ref[...]

    neg_c_softplus = -LRU_C * jax.nn.softplus(-lam_ref[...])
    for j in range(n_blk):
        sl = slice(j * blk, (j + 1) * blk)
        ri = ris[j]
        r = jax.nn.sigmoid(ri[:, :blk] + ba_ref[:, sl])
        i = jax.nn.sigmoid(ri[:, blk:] + bx_ref[:, sl])
        log_a = r * neg_c_softplus[:, sl]
        a = jnp.exp(log_a)
        a_buf[:, sl] = a
        one_minus_a2 = -jnp.tanh(log_a) * (a * a + 1.0)
        mult = jnp.where(one_minus_a2 > 0.0, one_minus_a2 * lax.rsqrt(one_minus_a2), 0.0)
        u_buf[:, sl] = mult * (i * xc[:, sl])

    hs = h_state[...]
    for t in range(steps):
        ts = slice(t * batch, (t + 1) * batch)
        hs = a_buf[ts, :] * hs + u_buf[ts, :]
        u_buf[ts, :] = hs
    h_state[...] = hs

    y_lru = (u_buf[...] * _gelu_tanh(g_lru)).astype(jnp.bfloat16)
    br_a = _dot(y_lru, wlru_ref[...])

    row = lax.broadcasted_iota(jnp.int32, (rows, 1), 0)
    pos = pid * steps + lax.shift_right_logical(row, batch.bit_length() - 1) + 1
    pooled = []
    for g, w in enumerate(POOL_WINDOWS):
        gs = slice(g * group_dim, (g + 1) * group_dim)
        cur = xp_buf[pool_tail:pool_tail + rows, gs]
        s = cur
        for k in range(1, w):
            off = pool_tail - k * batch
            s = s + xp_buf[off:off + rows, gs]
        inv_cnt = 1.0 / jnp.minimum(pos, w).astype(jnp.float32)
        pooled.append(s * inv_cnt - cur)
    xp_buf[0:pool_tail, :] = xp_buf[rows:rows + pool_tail, :]
    pb = jnp.concatenate(pooled, axis=-1).astype(jnp.bfloat16)
    y_pool = jnp.concatenate(
        [_dot(pb[:, j * blk:(j + 1) * blk], poolw_ref[j]) for j in range(d_pool // blk)], axis=-1)
    y_pool = (y_pool * pscale_ref[...]).astype(jnp.bfloat16)
    br_b = _dot(y_pool, wpool_ref[...])

    mix = jax.nn.sigmoid(gates[:, :d_model]) * br_a + jax.nn.sigmoid(gates[:, d_model:]) * br_b
    m = _dot(mix.astype(jnp.bfloat16), wo_ref[...])
    out = xt + m * _rms_scale(m) * gpost_ref[...]
    out_ref[...] = jnp.swapaxes(out.reshape(steps, batch, d_model), 0, 1)


def _mlp_kernel(x_ref, gpre_ref, gpost_ref, w1_ref, w2_ref, out_ref, hid_ref):
    d_ff = w1_ref.shape[1]
    sub_rows = x_ref.shape[0] // MLP_SUBTILES
    subs = [slice(s * sub_rows, (s + 1) * sub_rows) for s in range(MLP_SUBTILES)]
    xt = [x_ref[rs, :] for rs in subs]
    h = [(v * _rms_scale(v) * gpre_ref[...]).astype(jnp.bfloat16) for v in xt]
    for s, rs in enumerate(subs):
        for c in range(d_ff // MLP_FF_CHUNK):
            sl = slice(c * MLP_FF_CHUNK, (c + 1) * MLP_FF_CHUNK)
            a = jnp.maximum(_dot(h[s], w1_ref[:, sl]), 0.0)
            hid_ref[rs, sl] = (a * a).astype(jnp.bfloat16)
    f = [_dot(hid_ref[rs, :], w2_ref[...]) for rs in subs]
    for s, rs in enumerate(subs):
        out_ref[rs, :] = xt[s] + f[s] * _rms_scale(f[s]) * gpost_ref[...]


def _resident(shape):
    nd = len(shape)
    return pl.BlockSpec(shape, lambda i: (0,) * nd, pipeline_mode=pl.Buffered(1))


def _nbytes(shape, dtype):
    n = 1
    for s in shape:
        n *= s
    return n * jnp.dtype(dtype).itemsize


def _vmem_limit(resident, streamed, scratch, temporaries):
    need = sum(resident) + 2 * sum(streamed) + sum(scratch) + sum(temporaries)
    return min(need, V7X_VMEM_BYTES)


def _block_diag(w, per_block):
    n_heads, hd, _ = w.shape
    wg = w.reshape(n_heads // per_block, per_block, hd, hd)
    eye = jnp.eye(per_block, dtype=w.dtype)
    bd = jnp.einsum('ghij,hk->ghikj', wg, eye)
    return bd.reshape(n_heads // per_block, per_block * hd, per_block * hd)


def _mixer_call(x, p):
    batch, seq, d_model = x.shape
    steps = MIXER_TILE_STEPS
    rows = batch * steps
    assert batch == V7X_SUBLANES and seq % steps == 0
    d_rnn, d_pool = p["wlru"].shape[0], p["wpool"].shape[0]
    conv_tail = (CONV_WIDTH - 1) * batch
    pool_tail = (max(POOL_WINDOWS) - 1) * batch
    assert rows >= pool_tail

    weights = [p["gpre"], p["gpost"], p["win"], p["bgate"], p["convw"], p["convb"], p["wax"],
               p["ba"], p["bx"], p["lam"], p["poolw"], p["pscale"], p["wlru"], p["wpool"], p["wo"]]
    row_spec = pl.BlockSpec((batch, steps, d_model), lambda i: (0, i, 0))
    scratch = [((conv_tail + rows, d_rnn), jnp.float32), ((pool_tail + rows, d_pool), jnp.float32),
               ((rows, d_rnn), jnp.float32), ((rows, d_rnn), jnp.float32),
               ((batch, d_rnn), jnp.float32)]
    f32_row = _nbytes((rows, d_model), jnp.float32)
    vmem = _vmem_limit(
        resident=[_nbytes(w.shape, w.dtype) for w in weights],
        streamed=[f32_row, f32_row],
        scratch=[_nbytes(s, d) for s, d in scratch],
        temporaries=[8 * f32_row])
    return pl.pallas_call(
        functools.partial(_mixer_kernel, batch=batch, steps=steps),
        grid=(seq // steps,),
        in_specs=[row_spec] + [_resident(w.shape) for w in weights],
        out_specs=row_spec,
        out_shape=jax.ShapeDtypeStruct(x.shape, x.dtype),
        scratch_shapes=[pltpu.VMEM(s, d) for s, d in scratch],
        compiler_params=pltpu.CompilerParams(
            dimension_semantics=("arbitrary",), vmem_limit_bytes=vmem),
        name="mixer_call",
    )(x, *weights)


def _mlp_call(xs, gpre, gpost, w1, w2):
    n_rows, d_model = xs.shape
    rows = MLP_TILE_ROWS
    assert n_rows % rows == 0 and w1.shape[1] % MLP_FF_CHUNK == 0
    weights = [gpre, gpost, w1, w2]
    row_spec = pl.BlockSpec((rows, d_model), lambda i: (i, 0))
    hid = ((rows, w1.shape[1]), jnp.bfloat16)
    f32_row = _nbytes((rows, d_model), jnp.float32)
    vmem = _vmem_limit(
        resident=[_nbytes(w.shape, w.dtype) for w in weights],
        streamed=[f32_row, f32_row],
        scratch=[_nbytes(*hid)],
        temporaries=[_nbytes((rows, MLP_FF_CHUNK), jnp.float32), 2 * f32_row])
    return pl.pallas_call(
        _mlp_kernel,
        grid=(n_rows // rows,),
        in_specs=[row_spec] + [_resident(w.shape) for w in weights],
        out_specs=row_spec,
        out_shape=jax.ShapeDtypeStruct(xs.shape, xs.dtype),
        scratch_shapes=[pltpu.VMEM(*hid)],
        compiler_params=pltpu.CompilerParams(
            dimension_semantics=("arbitrary",), vmem_limit_bytes=vmem),
        name="mlp_call",
    )(xs, *weights)


def kernel(x, norm_mix_pre, norm_mix_post, norm_mlp_pre, norm_mlp_post, w_in, b_gate, conv_w, conv_b, lru_w_a, lru_b_a, lru_w_x, lru_b_x, lru_lambda, pool_w, pool_scale, w_lru_up, w_pool_up, w_o, w_ff1, w_ff2):
    batch, seq, d_model = x.shape
    depth = w_in.shape[0]
    bf16 = jnp.bfloat16
    heads_per_block = V7X_MXU_DIM // lru_w_a.shape[-1]
    groups_per_block = V7X_MXU_DIM // pool_w.shape[-1]
    row2 = lambda v: v.reshape(1, -1)

    for l in range(depth):
        wax = jnp.concatenate([_block_diag(lru_w_a[l], heads_per_block),
                               _block_diag(lru_w_x[l], heads_per_block)], axis=-1).astype(bf16)
        p = dict(gpre=row2(norm_mix_pre[l]), gpost=row2(norm_mix_post[l]), win=w_in[l].astype(bf16),
                 bgate=row2(b_gate[l]), convw=conv_w[l], convb=row2(conv_b[l]), wax=wax,
                 ba=row2(lru_b_a[l]), bx=row2(lru_b_x[l]), lam=row2(lru_lambda[l]),
                 poolw=_block_diag(pool_w[l], groups_per_block).astype(bf16),
                 pscale=row2(pool_scale[l]), wlru=w_lru_up[l].astype(bf16),
                 wpool=w_pool_up[l].astype(bf16), wo=w_o[l].astype(bf16))
        x = _mixer_call(x, p)
        x = _mlp_call(x.reshape(batch * seq, d_model), row2(norm_mlp_pre[l]), row2(norm_mlp_post[l]),
                      w_ff1[l].astype(bf16), w_ff2[l].astype(bf16)).reshape(batch, seq, d_model)
    return x
```

```python
import functools

import jax
import jax.numpy as jnp
from jax import lax
from jax.experimental import pallas as pl
from jax.experimental.pallas import tpu as pltpu

NORM_EPS = 1e-6
LRU_C = 8.0
CONV_WIDTH = 4
POOL_WINDOWS = (2, 4, 8, 16)
N_LRU_HEADS = 16

V7X_SUBLANES = 8
V7X_LANES = 128
V7X_MXU_DIM = 256
V7X_VMEM_BYTES = 64 * 1024 * 1024

MIXER_TILE_STEPS = 64
MLP_TILE_ROWS = 1024
MLP_SUBTILES = 2
MLP_FF_CHUNK = 1024
STAGE_SLOT_BYTES = 2 * 1024 * 1024


def _dot(a, b):
    return jnp.dot(a, b, preferred_element_type=jnp.float32)


def _gelu_tanh(v):
    c = 0.7978845608028654
    half = 0.5 * v
    return half * jnp.tanh(v * (c + (c * 0.044715) * (v * v))) + half


def _rms_scale(v):
    return lax.rsqrt(jnp.mean(v * v, axis=-1, keepdims=True) + NORM_EPS)


def _stage_weight(w_hbm, w_bf16, stage, sem):
    chunk = stage.shape[1]
    n_chunks = w_hbm.shape[0] // chunk

    def copy(c):
        return pltpu.make_async_copy(w_hbm.at[pl.ds(c * chunk, chunk), :], stage.at[c % 2], sem.at[c % 2])

    copy(0).start()
    for c in range(n_chunks):
        if c + 1 < n_chunks:
            copy(c + 1).start()
        copy(c).wait()
        w_bf16[c * chunk:(c + 1) * chunk, :] = stage[c % 2].astype(jnp.bfloat16)


def _mixer_kernel(x_ref, gpre_ref, gpost_ref, win_hbm, bgate_ref, convw_ref, convb_ref,
                  wax_ref, ba_ref, bx_ref, lam_ref, poolw_ref, pscale_ref, wlru_hbm,
                  wpool_hbm, wo_hbm, out_ref,
                  win_ref, wlru_ref, wpool_ref, wo_ref, stage_wide, stage_sq, stage_sem,
                  xl_buf, xp_buf, a_buf, u_buf, h_state, *, batch, steps):
    rows = batch * steps
    d_rnn = wlru_ref.shape[0]
    d_pool = wpool_ref.shape[0]
    d_model = x_ref.shape[-1]
    conv_tail = (CONV_WIDTH - 1) * batch
    pool_tail = (max(POOL_WINDOWS) - 1) * batch
    group_dim = d_pool // len(POOL_WINDOWS)
    pid = pl.program_id(0)

    @pl.when(pid == 0)
    def _():
        xl_buf[0:conv_tail, :] = jnp.zeros((conv_tail, d_rnn), jnp.float32)
        xp_buf[0:pool_tail, :] = jnp.zeros((pool_tail, d_pool), jnp.float32)
        h_state[...] = jnp.zeros_like(h_state)
        _stage_weight(win_hbm, win_ref, stage_wide, stage_sem)
        _stage_weight(wlru_hbm, wlru_ref, stage_sq, stage_sem)
        _stage_weight(wpool_hbm, wpool_ref, stage_sq, stage_sem)
        _stage_weight(wo_hbm, wo_ref, stage_sq, stage_sem)

    xt = jnp.swapaxes(x_ref[...], 0, 1).reshape(rows, d_model)
    h = (xt * _rms_scale(xt) * gpre_ref[...]).astype(jnp.bfloat16)

    xl_buf[conv_tail:conv_tail + rows, :] = _dot(h, win_ref[:, 0:d_rnn])
    xc = convb_ref[...]
    for k in range(CONV_WIDTH):
        xc = xc + xl_buf[k * batch:k * batch + rows, :] * convw_ref[k:k + 1, :]
    xl_buf[0:conv_tail, :] = xl_buf[rows:rows + conv_tail, :]
    xcb = xc.astype(jnp.bfloat16)
    blk = V7X_MXU_DIM
    n_blk = d_rnn // blk
    ris = [_dot(xcb[:, j * blk:(j + 1) * blk], wax_ref[j]) for j in range(n_blk)]

    g_lru = _dot(h, win_ref[:, d_rnn:2 * d_rnn])
    xp_buf[pool_tail:pool_tail + rows, :] = _dot(h, win_ref[:, 2 * d_rnn:2 * d_rnn + d_pool])
    gates = _dot(h, win_ref[:, 2 * d_rnn + d_pool:]) + bgate_ref[...]

    neg_c_softplus = -LRU_C * jax.nn.softplus(-lam_ref[...])
    for j in range(n_blk):
        sl = slice(j * blk, (j + 1) * blk)
        ri = ris[j]
        r = jax.nn.sigmoid(ri[:, :blk] + ba_ref[:, sl])
        i = jax.nn.sigmoid(ri[:, blk:] + bx_ref[:, sl])
        log_a = r * neg_c_softplus[:, sl]
        a = jnp.exp(log_a)
        a_buf[:, sl] = a
        one_minus_a2 = -jnp.tanh(log_a) * (a * a + 1.0)
        mult = jnp.where(one_minus_a2 > 0.0, one_minus_a2 * lax.rsqrt(one_minus_a2), 0.0)
        u_buf[:, sl] = mult * (i * xc[:, sl])

    hs = h_state[...]
    for t in range(steps):
        ts = slice(t * batch, (t + 1) * batch)
        hs = a_buf[ts, :] * hs + u_buf[ts, :]
        u_buf[ts, :] = hs
    h_state[...] = hs

    y_lru = (u_buf[...] * _gelu_tanh(g_lru)).astype(jnp.bfloat16)
    br_a = _dot(y_lru, wlru_ref[...])

    row = lax.broadcasted_iota(jnp.int32, (rows, 1), 0)
    pos = pid * steps + lax.shift_right_logical(row, batch.bit_length() - 1) + 1
    pooled = []
    for g, w in enumerate(POOL_WINDOWS):
        gs = slice(g * group_dim, (g + 1) * group_dim)
        cur = xp_buf[pool_tail:pool_tail + rows, gs]
        s = cur
        for k in range(1, w):
            off = pool_tail - k * batch
            s = s + xp_buf[off:off + rows, gs]
        inv_cnt = 1.0 / jnp.minimum(pos, w).astype(jnp.float32)
        pooled.append(s * inv_cnt - cur)
    xp_buf[0:pool_tail, :] = xp_buf[rows:rows + pool_tail, :]
    pb = jnp.concatenate(pooled, axis=-1).astype(jnp.bfloat16)
    y_pool = jnp.concatenate(
        [_dot(pb[:, j * blk:(j + 1) * blk], poolw_ref[j]) for j in range(d_pool // blk)], axis=-1)
    y_pool = (y_pool * pscale_ref[...]).astype(jnp.bfloat16)
    br_b = _dot(y_pool, wpool_ref[...])

    mix = jax.nn.sigmoid(gates[:, :d_model]) * br_a + jax.nn.sigmoid(gates[:, d_model:]) * br_b
    m = _dot(mix.astype(jnp.bfloat16), wo_ref[...])
    out = xt + m * _rms_scale(m) * gpost_ref[...]
    out_ref[...] = jnp.swapaxes(out.reshape(steps, batch, d_model), 0, 1)


def _mlp_kernel(x_ref, gpre_ref, gpost_ref, w1_hbm, w2_hbm, out_ref,
                w1_ref, w2_ref, stage_w1, stage_w2, stage_sem, hid_ref):
    @pl.when(pl.program_id(0) == 0)
    def _():
        _stage_weight(w1_hbm, w1_ref, stage_w1, stage_sem)
        _stage_weight(w2_hbm, w2_ref, stage_w2, stage_sem)

    d_ff = w1_ref.shape[1]
    sub_rows = x_ref.shape[0] // MLP_SUBTILES
    subs = [slice(s * sub_rows, (s + 1) * sub_rows) for s in range(MLP_SUBTILES)]
    xt = [x_ref[rs, :] for rs in subs]
    h = [(v * _rms_scale(v) * gpre_ref[...]).astype(jnp.bfloat16) for v in xt]
    for s, rs in enumerate(subs):
        for c in range(d_ff // MLP_FF_CHUNK):
            sl = slice(c * MLP_FF_CHUNK, (c + 1) * MLP_FF_CHUNK)
            a = jnp.maximum(_dot(h[s], w1_ref[:, sl]), 0.0)
            hid_ref[rs, sl] = (a * a).astype(jnp.bfloat16)
    f = [_dot(hid_ref[rs, :], w2_ref[...]) for rs in subs]
    for s, rs in enumerate(subs):
        out_ref[rs, :] = xt[s] + f[s] * _rms_scale(f[s]) * gpost_ref[...]


def _resident(shape):
    nd = len(shape)
    return pl.BlockSpec(shape, lambda i: (0,) * nd, pipeline_mode=pl.Buffered(1))


def _nbytes(shape, dtype):
    n = 1
    for s in shape:
        n *= s
    return n * jnp.dtype(dtype).itemsize


def _vmem_limit(resident, streamed, scratch, temporaries):
    need = sum(resident) + 2 * sum(streamed) + sum(scratch) + sum(temporaries)
    return min(need, V7X_VMEM_BYTES)


def _stage_shape(w):
    k, n = w.shape
    chunk = V7X_SUBLANES
    while 2 * chunk <= k and _nbytes((2 * chunk, n), jnp.float32) <= STAGE_SLOT_BYTES:
        chunk *= 2
    assert k % chunk == 0
    return (2, chunk, n)


def _block_diag(w, per_block):
    n_heads, hd, _ = w.shape
    wg = w.reshape(n_heads // per_block, per_block, hd, hd)
    eye = jnp.eye(per_block, dtype=w.dtype)
    bd = jnp.einsum('ghij,hk->ghikj', wg, eye)
    return bd.reshape(n_heads // per_block, per_block * hd, per_block * hd)


def _mixer_call(x, p):
    batch, seq, d_model = x.shape
    steps = MIXER_TILE_STEPS
    rows = batch * steps
    assert batch == V7X_SUBLANES and seq % steps == 0
    d_rnn, d_pool = p["wlru"].shape[0], p["wpool"].shape[0]
    conv_tail = (CONV_WIDTH - 1) * batch
    pool_tail = (max(POOL_WINDOWS) - 1) * batch
    assert rows >= pool_tail

    staged = ("win", "wlru", "wpool", "wo")
    stage_wide, stage_sq = _stage_shape(p["win"]), _stage_shape(p["wlru"])
    assert _stage_shape(p["wo"]) == stage_sq and _stage_shape(p["wpool"]) == stage_sq
    names = ["gpre", "gpost", "win", "bgate", "convw", "convb", "wax", "ba", "bx", "lam", "poolw",
             "pscale", "wlru", "wpool", "wo"]
    row_spec = pl.BlockSpec((batch, steps, d_model), lambda i: (0, i, 0))
    scratch = ([(p[n].shape, jnp.bfloat16) for n in staged]
               + [(stage_wide, jnp.float32), (stage_sq, jnp.float32)]
               + [((conv_tail + rows, d_rnn), jnp.float32), ((pool_tail + rows, d_pool), jnp.float32),
                  ((rows, d_rnn), jnp.float32), ((rows, d_rnn), jnp.float32),
                  ((batch, d_rnn), jnp.float32)])
    n_weight_scratch = len(staged) + 2
    f32_row = _nbytes((rows, d_model), jnp.float32)
    vmem = _vmem_limit(
        resident=[_nbytes(p[n].shape, p[n].dtype) for n in names if n not in staged],
        streamed=[f32_row, f32_row],
        scratch=[_nbytes(s, d) for s, d in scratch],
        temporaries=[8 * f32_row])
    return pl.pallas_call(
        functools.partial(_mixer_kernel, batch=batch, steps=steps),
        grid=(seq // steps,),
        in_specs=[row_spec] + [pl.BlockSpec(memory_space=pl.ANY) if n in staged
                               else _resident(p[n].shape) for n in names],
        out_specs=row_spec,
        out_shape=jax.ShapeDtypeStruct(x.shape, x.dtype),
        scratch_shapes=([pltpu.VMEM(s, d) for s, d in scratch[:n_weight_scratch]]
                        + [pltpu.SemaphoreType.DMA((2,))]
                        + [pltpu.VMEM(s, d) for s, d in scratch[n_weight_scratch:]]),
        compiler_params=pltpu.CompilerParams(
            dimension_semantics=("arbitrary",), vmem_limit_bytes=vmem),
        name="mixer_call",
    )(x, *[p[n] for n in names])


def _mlp_call(xs, gpre, gpost, w1, w2):
    n_rows, d_model = xs.shape
    rows = MLP_TILE_ROWS
    assert n_rows % rows == 0 and w1.shape[1] % MLP_FF_CHUNK == 0
    row_spec = pl.BlockSpec((rows, d_model), lambda i: (i, 0))
    hbm_spec = pl.BlockSpec(memory_space=pl.ANY)
    weight_scratch = [(w1.shape, jnp.bfloat16), (w2.shape, jnp.bfloat16),
                      (_stage_shape(w1), jnp.float32), (_stage_shape(w2), jnp.float32)]
    hid = ((rows, w1.shape[1]), jnp.bfloat16)
    f32_row = _nbytes((rows, d_model), jnp.float32)
    vmem = _vmem_limit(
        resident=[_nbytes(gpre.shape, gpre.dtype), _nbytes(gpost.shape, gpost.dtype)],
        streamed=[f32_row, f32_row],
        scratch=[_nbytes(s, d) for s, d in weight_scratch + [hid]],
        temporaries=[_nbytes((rows, MLP_FF_CHUNK), jnp.float32), 2 * f32_row])
    return pl.pallas_call(
        _mlp_kernel,
        grid=(n_rows // rows,),
        in_specs=[row_spec, _resident(gpre.shape), _resident(gpost.shape), hbm_spec, hbm_spec],
        out_specs=row_spec,
        out_shape=jax.ShapeDtypeStruct(xs.shape, xs.dtype),
        scratch_shapes=([pltpu.VMEM(s, d) for s, d in weight_scratch]
                        + [pltpu.SemaphoreType.DMA((2,)), pltpu.VMEM(*hid)]),
        compiler_params=pltpu.CompilerParams(
            dimension_semantics=("arbitrary",), vmem_limit_bytes=vmem),
        name="mlp_call",
    )(xs, gpre, gpost, w1, w2)


def kernel(x, norm_mix_pre, norm_mix_post, norm_mlp_pre, norm_mlp_post, w_in, b_gate, conv_w, conv_b, lru_w_a, lru_b_a, lru_w_x, lru_b_x, lru_lambda, pool_w, pool_scale, w_lru_up, w_pool_up, w_o, w_ff1, w_ff2):
    batch, seq, d_model = x.shape
    depth = w_in.shape[0]
    bf16 = jnp.bfloat16
    heads_per_block = V7X_MXU_DIM // lru_w_a.shape[-1]
    groups_per_block = V7X_MXU_DIM // pool_w.shape[-1]
    row2 = lambda v: v.reshape(1, -1)

    for l in range(depth):
        wax = jnp.concatenate([_block_diag(lru_w_a[l], heads_per_block),
                               _block_diag(lru_w_x[l], heads_per_block)], axis=-1).astype(bf16)
        p = dict(gpre=row2(norm_mix_pre[l]), gpost=row2(norm_mix_post[l]), win=w_in[l],
                 bgate=row2(b_gate[l]), convw=conv_w[l], convb=row2(conv_b[l]), wax=wax,
                 ba=row2(lru_b_a[l]), bx=row2(lru_b_x[l]), lam=row2(lru_lambda[l]),
                 poolw=_block_diag(pool_w[l], groups_per_block).astype(bf16),
                 pscale=row2(pool_scale[l]), wlru=w_lru_up[l], wpool=w_pool_up[l], wo=w_o[l])
        x = _mixer_call(x, p)
        x = _mlp_call(x.reshape(batch * seq, d_model), row2(norm_mlp_pre[l]), row2(norm_mlp_post[l]),
                      w_ff1[l], w_ff2[l]).reshape(batch, seq, d_model)
    return x
```
